```python
import jax, jax.numpy as jnp
from jax import lax
import numpy as np

D_MODEL = 2048
BATCH = 1
SEQ = 16384
DEPTH = 2

MIX_WIDTH = D_MODEL
CONV_CH = MIX_WIDTH // 2
CONV_GROUPS = 8
CONV_WIDTH = 31
GDN_HEADS = 8
HEAD_DIM = (MIX_WIDTH - CONV_CH) // GDN_HEADS
GDN_WIDTH = GDN_HEADS * HEAD_DIM
SHORT_CONV = 4
CHUNK = 64
D_FF = ((8 * D_MODEL // 3 + 255) // 256) * 256
IN_COLS = 2 * CONV_CH + 4 * GDN_WIDTH + 2 * GDN_HEADS
LN_EPS = 1e-5
RMS_EPS = 1e-6
L2_EPS = 1e-6

kernel_name = "hymba_conformer_gdn_deepnorm"


def layer_norm(x, g, b):
    xf = x.astype(jnp.float32)
    mu = jnp.mean(xf, -1, keepdims=True)
    var = jnp.mean(jnp.square(xf - mu), -1, keepdims=True)
    y = (xf - mu) * lax.rsqrt(var + LN_EPS) * g.astype(jnp.float32) + b.astype(jnp.float32)
    return y.astype(x.dtype)


def rms_norm(x, g):
    xf = x.astype(jnp.float32)
    return xf * lax.rsqrt(jnp.mean(xf * xf, -1, keepdims=True) + RMS_EPS) * g.astype(jnp.float32)


def l2norm(x):
    xf = x.astype(jnp.float32)
    return xf * lax.rsqrt(jnp.sum(xf * xf, -1, keepdims=True) + L2_EPS)


def causal_depthwise_conv(x, w):
    k_len, ch = w.shape
    return lax.conv_general_dilated(
        x, w.astype(x.dtype)[:, None, :], window_strides=(1,),
        padding=[(k_len - 1, 0)], dimension_numbers=('NWC', 'WIO', 'NWC'),
        feature_group_count=ch)


def conformer_conv_group(u_val, u_gate, dw_w, dw_b, ln_g, ln_b):
    h = u_val * jax.nn.sigmoid(u_gate)
    h = causal_depthwise_conv(h, dw_w) + dw_b.astype(h.dtype)
    h = layer_norm(h, ln_g, ln_b)
    return jax.nn.silu(h)


def chunk_gated_delta_rule(q, k, v, g, beta):
    bsz, seq, nh, dk = q.shape
    dv = v.shape[-1]
    n_chunks = seq // CHUNK
    q = q * (dk ** -0.5)

    def chunks(t):
        return t.reshape(bsz, n_chunks, CHUNK, nh, -1).transpose(0, 3, 1, 2, 4)

    q, k, v = chunks(q), chunks(k), chunks(v)
    beta = beta.reshape(bsz, n_chunks, CHUNK, nh).transpose(0, 3, 1, 2)
    g = jnp.cumsum(g.reshape(bsz, n_chunks, CHUNK, nh).transpose(0, 3, 1, 2), axis=-1)

    idx = jnp.arange(CHUNK)
    causal = idx[:, None] >= idx[None, :]
    strict = idx[:, None] > idx[None, :]
    decay = jnp.exp(jnp.where(causal, g[..., :, None] - g[..., None, :], -jnp.inf))

    k_beta = k * beta[..., None]
    v_beta = v * beta[..., None]
    L = jnp.einsum('bhnid,bhnjd->bhnij', k_beta, k) * jnp.where(strict, decay, 0.0)
    lhs = jnp.eye(CHUNK, dtype=jnp.float32) + L
    rhs = jnp.concatenate([v_beta, k_beta * jnp.exp(g)[..., None]], axis=-1)
    sol = lax.linalg.triangular_solve(lhs, rhs, left_side=True, lower=True, unit_diagonal=True)
    u_c = sol[..., :dv]
    w_c = sol[..., dv:]

    intra = jnp.einsum('bhnid,bhnjd->bhnij', q, k) * decay
    q_dec = q * jnp.exp(g)[..., None]
    g_last = g[..., -1]
    k_dec = k * jnp.exp(g_last[..., None] - g)[..., None]

    def step(state, xs):
        u_i, w_i, qd_i, kd_i, a_i, gl_i = xs
        v_new = u_i - jnp.einsum('bhcd,bhde->bhce', w_i, state)
        o_i = (jnp.einsum('bhcd,bhde->bhce', qd_i, state)
               + jnp.einsum('bhij,bhje->bhie', a_i, v_new))
        state = state * jnp.exp(gl_i)[..., None, None] + jnp.einsum('bhcd,bhce->bhde', kd_i, v_new)
        return state, o_i

    xs = tuple(jnp.moveaxis(t, 2, 0) for t in (u_c, w_c, q_dec, k_dec, intra, g_last))
    state0 = jnp.zeros((bsz, nh, dk, dv), jnp.float32)
    _, o = lax.scan(step, state0, xs)
    return o.transpose(1, 0, 3, 2, 4).reshape(bsz, seq, nh, dv)


def gated_deltanet_group(qkv, z, b_raw, a_raw, sc_w, A_log, dt_bias, norm_w):
    bsz, seq, _ = qkv.shape
    qkv = jax.nn.silu(causal_depthwise_conv(qkv, sc_w))
    q = qkv[..., :GDN_WIDTH].reshape(bsz, seq, GDN_HEADS, HEAD_DIM)
    k = qkv[..., GDN_WIDTH:2 * GDN_WIDTH].reshape(bsz, seq, GDN_HEADS, HEAD_DIM)
    v = qkv[..., 2 * GDN_WIDTH:].reshape(bsz, seq, GDN_HEADS, HEAD_DIM).astype(jnp.float32)
    q, k = l2norm(q), l2norm(k)
    beta = jax.nn.sigmoid(b_raw.astype(jnp.float32))
    g = -jnp.exp(A_log.astype(jnp.float32)) * jax.nn.softplus(
        a_raw.astype(jnp.float32) + dt_bias.astype(jnp.float32))
    o = chunk_gated_delta_rule(q, k, v, g, beta)
    zh = z.reshape(bsz, seq, GDN_HEADS, HEAD_DIM).astype(jnp.float32)
    o = rms_norm(o, norm_w) * jax.nn.silu(zh)
    return o.reshape(bsz, seq, GDN_WIDTH).astype(z.dtype)


def setup_inputs(seed: int = 0) -> dict:
    key = jax.random.key(seed)
    ks = jax.random.split(key, 20)
    f32 = jnp.float32
    beta_init = (8 * DEPTH) ** -0.25
    nrm = lambda k, shape, s: jax.random.normal(k, shape, f32) * s
    dt = jnp.exp(jax.random.uniform(ks[8], (DEPTH, GDN_HEADS), f32, np.log(1e-3), np.log(1e-1)))
    return {
        "x": jax.random.normal(ks[0], (BATCH, SEQ, D_MODEL), f32),
        "w_in": nrm(ks[1], (DEPTH, D_MODEL, IN_COLS), D_MODEL ** -0.5),
        "conv_dw_w": nrm(ks[2], (DEPTH, CONV_WIDTH, CONV_CH), CONV_WIDTH ** -0.5),
        "conv_dw_b": nrm(ks[3], (DEPTH, CONV_CH), 0.01),
        "conv_ln_g": 1.0 + nrm(ks[4], (DEPTH, CONV_CH), 0.02),
        "conv_ln_b": nrm(ks[5], (DEPTH, CONV_CH), 0.01),
        "gdn_conv_w": nrm(ks[6], (DEPTH, SHORT_CONV, 3 * GDN_WIDTH), SHORT_CONV ** -0.5),
        "gdn_A_log": jnp.log(jax.random.uniform(ks[7], (DEPTH, GDN_HEADS), f32, 1.0, 16.0)),
        "gdn_dt_bias": dt + jnp.log(-jnp.expm1(-dt)),
        "gdn_norm_w": 1.0 + nrm(ks[9], (DEPTH, HEAD_DIM), 0.02),
        "w_out": nrm(ks[10], (DEPTH, MIX_WIDTH, D_MODEL), MIX_WIDTH ** -0.5 * beta_init),
        "ln1_g": 1.0 + nrm(ks[11], (DEPTH, D_MODEL), 0.02),
        "ln1_b": nrm(ks[12], (DEPTH, D_MODEL), 0.01),
        "w_gate_up": nrm(ks[13], (DEPTH, D_MODEL, 2 * D_FF), D_MODEL ** -0.5),
        "w_down": nrm(ks[14], (DEPTH, D_FF, D_MODEL), D_FF ** -0.5 * beta_init),
        "ln2_g": 1.0 + nrm(ks[15], (DEPTH, D_MODEL), 0.02),
        "ln2_b": nrm(ks[16], (DEPTH, D_MODEL), 0.01),
    }


def reference(x, w_in, conv_dw_w, conv_dw_b, conv_ln_g, conv_ln_b, gdn_conv_w, gdn_A_log,
              gdn_dt_bias, gdn_norm_w, w_out, ln1_g, ln1_b, w_gate_up, w_down, ln2_g, ln2_b):
    alpha = (2 * DEPTH) ** 0.25
    c0 = 2 * CONV_CH
    c1 = c0 + 3 * GDN_WIDTH
    c2 = c1 + GDN_WIDTH
    c3 = c2 + GDN_HEADS
    for l in range(DEPTH):
        u = jnp.einsum('bsd,dc->bsc', x, w_in[l])
        conv_out = conformer_conv_group(u[..., :CONV_CH], u[..., CONV_CH:c0],
                                        conv_dw_w[l], conv_dw_b[l], conv_ln_g[l], conv_ln_b[l])
        gdn_out = gated_deltanet_group(u[..., c0:c1], u[..., c1:c2], u[..., c2:c3], u[..., c3:],
                                       gdn_conv_w[l], gdn_A_log[l], gdn_dt_bias[l], gdn_norm_w[l])
        mix = jnp.einsum('bsm,md->bsd', jnp.concatenate([conv_out, gdn_out], axis=-1), w_out[l])
        x = layer_norm(alpha * x + mix, ln1_g[l], ln1_b[l])
        gu = jnp.einsum('bsd,df->bsf', x, w_gate_up[l])
        hid = jax.nn.silu(gu[..., :D_FF]) * gu[..., D_FF:]
        ffn = jnp.einsum('bsf,fd->bsd', hid, w_down[l])
        x = layer_norm(alpha * x + ffn, ln2_g[l], ln2_b[l])
    return x
```

```python
import functools

import jax
import jax.numpy as jnp
from jax import lax
from jax.experimental import pallas as pl
from jax.experimental.pallas import tpu as pltpu

F32 = jnp.float32
BF16 = jnp.bfloat16

LANES = 128
CONV_WIDTH = 31
SHORT_CONV = 4
GDN_HEADS = 8
HEAD_DIM = 128
CHUNK = 64
PAIR = 2 * CHUNK
LN_EPS = 1e-5
RMS_EPS = 1e-6
L2_EPS = 1e-6
HALO = 32
SHORT_HALO = 8

VMEM_LIMIT = 56 * 1024 * 1024


def _sigmoid(x):
    return 1.0 / (1.0 + jnp.exp(-x))


def _silu(x):
    return x * _sigmoid(x)


def _softplus(x):
    return jnp.maximum(x, 0.0) + jnp.log(1.0 + jnp.exp(-jnp.abs(x)))


def _layer_norm(y, g, b):
    mu = jnp.mean(y, axis=-1, keepdims=True)
    d = y - mu
    var = jnp.mean(d * d, axis=-1, keepdims=True)
    return d * lax.rsqrt(var + LN_EPS) * g + b


def _bmm(a, b):
    return lax.dot_general(a.astype(BF16), b.astype(BF16), (((2,), (1,)), ((0,), (0,))),
                           preferred_element_type=F32)


def _bmm_nt(a, b):
    return lax.dot_general(a.astype(BF16), b.astype(BF16), (((2,), (2,)), ((0,), (0,))),
                           preferred_element_type=F32)


def _in_proj_kernel(x_ref, w_ref, o_ref):
    o_ref[...] = jnp.dot(x_ref[...].astype(BF16), w_ref[...], preferred_element_type=F32)


def _in_proj(x, w, tm=1024, tn=896):
    s, d = x.shape
    n = w.shape[1]
    assert s % tm == 0 and n % tn == 0
    return pl.pallas_call(
        _in_proj_kernel,
        grid=(s // tm, n // tn),
        in_specs=[pl.BlockSpec((tm, d), lambda i, j: (i, 0)),
                  pl.BlockSpec((d, tn), lambda i, j: (0, j))],
        out_specs=pl.BlockSpec((tm, tn), lambda i, j: (i, j)),
        out_shape=jax.ShapeDtypeStruct((s, n), F32),
        compiler_params=pltpu.CompilerParams(
            dimension_semantics=("parallel", "parallel"), vmem_limit_bytes=VMEM_LIMIT),
        name="in_proj",
    )(x, w)


def _conformer_kernel(val_ref, gate_ref, w_ref, b_ref, g_ref, beta_ref, o_ref, hbuf, cbuf, *, rows):
    i = pl.program_id(0)
    t, ch = val_ref.shape

    @pl.when(i == 0)
    def _():
        hbuf[0:HALO, :] = jnp.zeros((HALO, ch), F32)

    @pl.when(i > 0)
    def _():
        hbuf[0:HALO, :] = hbuf[t:t + HALO, :]

    hbuf[HALO:HALO + t, :] = val_ref[...] * _sigmoid(gate_ref[...])

    base = HALO - (CONV_WIDTH - 1)
    for cb in range(ch // LANES):
        cols = slice(cb * LANES, (cb + 1) * LANES)

        for r0 in range(0, t, rows):
            acc = jnp.broadcast_to(b_ref[:, cols], (rows, LANES))
            for k in range(CONV_WIDTH):
                acc = acc + w_ref[k:k + 1, cols] * hbuf[r0 + base + k:r0 + base + k + rows, cols]
            cbuf[r0:r0 + rows, cols] = acc

    y = _layer_norm(cbuf[...], g_ref[...], beta_ref[...])
    o_ref[...] = _silu(y).astype(o_ref.dtype)


def _conformer(u, dw_w, dw_b, ln_g, ln_b, t=256, rows=64):
    s = u.shape[0]
    ch = dw_w.shape[1]
    row = lambda a: a.reshape(1, ch).astype(F32)
    full = lambda shape: pl.BlockSpec(shape, lambda i: (0, 0))
    return pl.pallas_call(
        functools.partial(_conformer_kernel, rows=rows),
        grid=(s // t,),
        in_specs=[pl.BlockSpec((t, ch), lambda i: (i, 0)),
                  pl.BlockSpec((t, ch), lambda i: (i, 1)),
                  full((CONV_WIDTH, ch)), full((1, ch)), full((1, ch)), full((1, ch))],
        out_specs=pl.BlockSpec((t, ch), lambda i: (i, 0)),
        out_shape=jax.ShapeDtypeStruct((s, ch), BF16),
        scratch_shapes=[pltpu.VMEM((t + HALO, ch), F32), pltpu.VMEM((t, ch), F32)],
        compiler_params=pltpu.CompilerParams(
            dimension_semantics=("arbitrary",), vmem_limit_bytes=VMEM_LIMIT),
        name="conformer",
    )(u, u, dw_w.astype(F32), row(dw_b), row(ln_g), row(ln_b))


def _gdn_pre_kernel(alog_ref, dtb_ref, q_ref, k_ref, v_ref, ba_ref, wq_ref, wk_ref, wv_ref,
                    u_ref, w_ref, qd_ref, kdt_ref, intra_ref, egl_ref, xbuf):
    h = pl.program_id(0)
    i = pl.program_id(1)
    t = q_ref.shape[0]
    nb = t // PAIR

    @pl.when(i == 0)
    def _():
        xbuf[:, 0:SHORT_HALO, :] = jnp.zeros((3, SHORT_HALO, LANES), F32)

    @pl.when(i > 0)
    def _():
        xbuf[:, 0:SHORT_HALO, :] = xbuf[:, t:t + SHORT_HALO, :]

    xbuf[0, SHORT_HALO:SHORT_HALO + t, :] = q_ref[...]
    xbuf[1, SHORT_HALO:SHORT_HALO + t, :] = k_ref[...]
    xbuf[2, SHORT_HALO:SHORT_HALO + t, :] = v_ref[...]

    def conv_silu(j, wref):
        base = SHORT_HALO - (SHORT_CONV - 1)
        acc = wref[0:1, :] * xbuf[j, pl.ds(base, t), :]
        for kk in range(1, SHORT_CONV):
            acc = acc + wref[kk:kk + 1, :] * xbuf[j, pl.ds(base + kk, t), :]
        return _silu(acc)

    def l2n(a):
        return a * lax.rsqrt(jnp.sum(a * a, axis=-1, keepdims=True) + L2_EPS)

    qn = l2n(conv_silu(0, wq_ref)) * (HEAD_DIM ** -0.5)
    kn = l2n(conv_silu(1, wk_ref))
    v = conv_silu(2, wv_ref)

    ba = ba_ref[...]
    lane = lax.broadcasted_iota(jnp.int32, (t, LANES), 1)
    b_col = jnp.sum(jnp.where(lane == h, ba, 0.0), axis=-1, keepdims=True)
    a_col = jnp.sum(jnp.where(lane == h + GDN_HEADS, ba, 0.0), axis=-1, keepdims=True)
    beta = _sigmoid(b_col)
    neg_a = -jnp.exp(jnp.full((t, 1), alog_ref[h], F32))
    g = neg_a * _softplus(a_col + dtb_ref[h])

    gc = jnp.broadcast_to(g, (t, LANES))
    rin = lax.broadcasted_iota(jnp.int32, (t, LANES), 0) & (CHUNK - 1)
    d = 1
    while d < CHUNK:
        gc = gc + jnp.where(rin >= d, pltpu.roll(gc, d, 0), 0.0)
        d *= 2

    gc3 = gc.reshape(t // CHUNK, CHUNK, LANES)
    gl3 = gc3[:, CHUNK - 1:CHUNK, :]
    kfac = jnp.exp(gl3 - gc3).reshape(t, LANES)
    egl_ref[...] = jnp.exp(gl3).reshape(t // CHUNK, LANES)
    eg = jnp.exp(gc)

    kb = kn * beta
    vb = v * beta
    kbe = kb * eg
    qd_ref[...] = (qn * eg).astype(qd_ref.dtype)
    kdec = kn * kfac

    blk = lambda a: a.reshape(nb, PAIR, a.shape[-1])
    gcb = blk(gc)
    gct = jnp.stack([gc[b * PAIR:(b + 1) * PAIR, :].T for b in range(nb)], axis=0)
    ii = lax.broadcasted_iota(jnp.int32, (PAIR, PAIR), 0)
    jj = lax.broadcasted_iota(jnp.int32, (PAIR, PAIR), 1)
    same = (ii >= CHUNK) == (jj >= CHUNK)
    causal = same & (ii >= jj)
    strict = same & (ii > jj)
    dec = jnp.exp(jnp.where(causal, gcb - gct, -jnp.inf))

    kn_b = blk(kn).astype(BF16)
    kk = _bmm_nt(blk(kb), kn_b)
    qk = _bmm_nt(blk(qn), kn_b)
    intra = qk * dec
    intra_ref[...] = intra.reshape(t, PAIR).astype(intra_ref.dtype)

    n = -(kk * jnp.where(strict, dec, 0.0))
    eye = (ii == jj).astype(F32)
    p = eye + n
    npow = n
    steps = 1
    while 2 * steps < CHUNK:
        npow = _bmm(npow, npow)
        p = p + _bmm(p, npow)
        steps *= 2

    rhs = jnp.concatenate([blk(vb), blk(kbe)], axis=-1)
    sol = _bmm(p, rhs)
    u_ref[...] = sol[:, :, :HEAD_DIM].reshape(t, HEAD_DIM)
    w_ref[...] = sol[:, :, HEAD_DIM:].reshape(t, HEAD_DIM).astype(w_ref.dtype)

    for b in range(nb):
        kdt_ref[b * PAIR:(b + 1) * PAIR, :] = kdec[b * PAIR:(b + 1) * PAIR, :].T.astype(kdt_ref.dtype)


def _gdn_pre(u, conv_w, a_log, dt_bias, t=512):
    s = u.shape[0]
    nh = GDN_HEADS
    q0 = 2 * 1024 // LANES
    k0 = q0 + nh
    v0 = k0 + nh
    ba0 = q0 + 4 * nh
    smem = pl.BlockSpec(memory_space=pltpu.SMEM)
    col = lambda c0: pl.BlockSpec((t, LANES), lambda h, i: (i, c0 + h))
    wcol = lambda c0: pl.BlockSpec((SHORT_CONV, LANES), lambda h, i: (0, c0 + h))
    out = pl.BlockSpec((t, LANES), lambda h, i: (i, h))
    width = nh * HEAD_DIM
    return pl.pallas_call(
        _gdn_pre_kernel,
        grid=(nh, s // t),
        in_specs=[smem, smem, col(q0), col(k0), col(v0),
                  pl.BlockSpec((t, LANES), lambda h, i: (i, ba0)),
                  wcol(0), wcol(nh), wcol(2 * nh)],
        out_specs=[out, out, out, out, out,
                   pl.BlockSpec((t // CHUNK, LANES), lambda h, i: (i, h))],
        out_shape=[jax.ShapeDtypeStruct((s, width), F32),
                   jax.ShapeDtypeStruct((s, width), BF16),
                   jax.ShapeDtypeStruct((s, width), BF16),
                   jax.ShapeDtypeStruct((s, width), BF16),
                   jax.ShapeDtypeStruct((s, width), BF16),
                   jax.ShapeDtypeStruct((s // CHUNK, width), F32)],
        scratch_shapes=[pltpu.VMEM((3, t + SHORT_HALO, LANES), F32)],
        compiler_params=pltpu.CompilerParams(
            dimension_semantics=("arbitrary", "arbitrary"), vmem_limit_bytes=VMEM_LIMIT),
        name="gdn_pre",
    )(a_log.astype(F32), dt_bias.astype(F32), u, u, u, u,
      conv_w.astype(F32), conv_w.astype(F32), conv_w.astype(F32))


def _gdn_rec_kernel(u_ref, w_ref, qd_ref, kdt_ref, intra_ref, egl_ref, z_ref, nw_ref, o_ref,
                    s_ref, obuf):
    i = pl.program_id(0)
    t = u_ref.shape[0]

    @pl.when(i == 0)
    def _():
        s_ref[...] = jnp.zeros(s_ref.shape, F32)

    zeros = jnp.zeros((CHUNK, HEAD_DIM), BF16)

    for p in range(t // PAIR):
        pr = p * PAIR
        prow = slice(pr, pr + PAIR)
        for e in range(2):
            rows = slice(pr + e * CHUNK, pr + (e + 1) * CHUNK)
            for h in range(GDN_HEADS):
                cols = slice(h * HEAD_DIM, (h + 1) * HEAD_DIM)
                state = s_ref[h]
                wq = jnp.concatenate([w_ref[rows, cols], qd_ref[rows, cols]], axis=0)
                r = jnp.dot(wq, state.astype(BF16), preferred_element_type=F32)
                v_new = (u_ref[rows, cols] - r[:CHUNK]).astype(BF16)
                v_pair = jnp.concatenate([v_new, zeros] if e == 0 else [zeros, v_new], axis=0)
                obuf[rows, cols] = r[CHUNK:] + jnp.dot(intra_ref[rows, cols], v_pair,
                                                       preferred_element_type=F32)
                decay = egl_ref[2 * p + e:2 * p + e + 1, cols]
                s_ref[h] = state * decay + jnp.dot(kdt_ref[prow, cols], v_pair,
                                                   preferred_element_type=F32)

    for h in range(GDN_HEADS):
        cols = slice(h * HEAD_DIM, (h + 1) * HEAD_DIM)
        o = obuf[:, cols]
        y = o * lax.rsqrt(jnp.mean(o * o, axis=-1, keepdims=True) + RMS_EPS) * nw_ref[...]
        o_ref[:, cols] = (y * _silu(z_ref[:, cols])).astype(o_ref.dtype)


def _gdn_rec(ug, wg, qd, kdt, intra, egl, u, norm_w, t=512):
    s, width = ug.shape
    z0 = (2 * 1024 + 3 * width) // width
    blk = pl.BlockSpec((t, width), lambda i: (i, 0))
    return pl.pallas_call(
        _gdn_rec_kernel,
        grid=(s // t,),
        in_specs=[blk, blk, blk, blk, blk,
                  pl.BlockSpec((t // CHUNK, width), lambda i: (i, 0)),
                  pl.BlockSpec((t, width), lambda i: (i, z0)),
                  pl.BlockSpec((1, HEAD_DIM), lambda i: (0, 0))],
        out_specs=blk,
        out_shape=jax.ShapeDtypeStruct((s, width), BF16),
        scratch_shapes=[pltpu.VMEM((GDN_HEADS, HEAD_DIM, HEAD_DIM), F32),
                        pltpu.VMEM((t, width), F32)],
        compiler_params=pltpu.CompilerParams(
            dimension_semantics=("arbitrary",), vmem_limit_bytes=VMEM_LIMIT),
        name="gdn_rec",
    )(ug, wg, qd, kdt, intra, egl, u, norm_w.reshape(1, HEAD_DIM).astype(F32))


def _out_ln_kernel(c_ref, g_ref, x_ref, w_ref, lg_ref, lb_ref, o_ref, *, alpha):
    half = c_ref.shape[1]
    mix = jnp.dot(c_ref[...], w_ref[0:half, :], preferred_element_type=F32)
    mix = mix + jnp.dot(g_ref[...], w_ref[half:2 * half, :], preferred_element_type=F32)
    o_ref[...] = _layer_norm(alpha * x_ref[...] + mix, lg_ref[...], lb_ref[...])


def _out_ln(conv_out, gdn_out, x, w_out, ln_g, ln_b, alpha, tm=512):
    s, d = x.shape
    half = conv_out.shape[1]
    row = lambda a: a.reshape(1, d).astype(F32)
    return pl.pallas_call(
        functools.partial(_out_ln_kernel, alpha=alpha),
        grid=(s // tm,),
        in_specs=[pl.BlockSpec((tm, half), lambda i: (i, 0)),
                  pl.BlockSpec((tm, half), lambda i: (i, 0)),
                  pl.BlockSpec((tm, d), lambda i: (i, 0)),
                  pl.BlockSpec((2 * half, d), lambda i: (0, 0)),
                  pl.BlockSpec((1, d), lambda i: (0, 0)),
                  pl.BlockSpec((1, d), lambda i: (0, 0))],
        out_specs=pl.BlockSpec((tm, d), lambda i: (i, 0)),
        out_shape=jax.ShapeDtypeStruct((s, d), F32),
        compiler_params=pltpu.CompilerParams(
            dimension_semantics=("parallel",), vmem_limit_bytes=VMEM_LIMIT),
        name="out_ln",
    )(conv_out, gdn_out, x, w_out, row(ln_g), row(ln_b))


def _ffn_ln_kernel(x_ref, wg_ref, wu_ref, wd_ref, lg_ref, lb_ref, o_ref, xb_ref, acc_ref, *, alpha):
    f = pl.program_id(1)

    @pl.when(f == 0)
    def _():
        xb_ref[...] = x_ref[...].astype(BF16)
        acc_ref[...] = jnp.zeros(acc_ref.shape, F32)

    xb = xb_ref[...]
    gate = jnp.dot(xb, wg_ref[...], preferred_element_type=F32)
    up = jnp.dot(xb, wu_ref[...], preferred_element_type=F32)
    hid = (_silu(gate) * up).astype(BF16)
    acc_ref[...] += jnp.dot(hid, wd_ref[...], preferred_element_type=F32)

    @pl.when(f == pl.num_programs(1) - 1)
    def _():
        o_ref[...] = _layer_norm(alpha * x_ref[...] + acc_ref[...], lg_ref[...], lb_ref[...])


def _ffn_ln(x, w_gate_up, w_down, ln_g, ln_b, alpha, tm=512, tf=512):
    s, d = x.shape
    dff = w_down.shape[0]
    nf = dff // tf
    row = lambda a: a.reshape(1, d).astype(F32)
    return pl.pallas_call(
        functools.partial(_ffn_ln_kernel, alpha=alpha),
        grid=(s // tm, nf),
        in_specs=[pl.BlockSpec((tm, d), lambda i, f: (i, 0)),
                  pl.BlockSpec((d, tf), lambda i, f: (0, f)),
                  pl.BlockSpec((d, tf), lambda i, f: (0, nf + f)),
                  pl.BlockSpec((tf, d), lambda i, f: (f, 0)),
                  pl.BlockSpec((1, d), lambda i, f: (0, 0)),
                  pl.BlockSpec((1, d), lambda i, f: (0, 0))],
        out_specs=pl.BlockSpec((tm, d), lambda i, f: (i, 0)),
        out_shape=jax.ShapeDtypeStruct((s, d), F32),
        scratch_shapes=[pltpu.VMEM((tm, d), BF16), pltpu.VMEM((tm, d), F32)],
        compiler_params=pltpu.CompilerParams(
            dimension_semantics=("parallel", "arbitrary"), vmem_limit_bytes=VMEM_LIMIT),
        name="ffn_ln",
    )(x, w_gate_up, w_gate_up, w_down, row(ln_g), row(ln_b))


def kernel(x, w_in, conv_dw_w, conv_dw_b, conv_ln_g, conv_ln_b, gdn_conv_w, gdn_A_log, gdn_dt_bias,
           gdn_norm_w, w_out, ln1_g, ln1_b, w_gate_up, w_down, ln2_g, ln2_b):
    bsz, seq, d = x.shape
    assert bsz == 1
    depth = w_in.shape[0]
    alpha = (2 * depth) ** 0.25
    in_cols = w_in.shape[2]
    in_cols_pad = -(-in_cols // (7 * LANES)) * (7 * LANES)

    h = x.reshape(seq, d)
    for l in range(depth):
        w_in_l = jnp.pad(w_in[l].astype(BF16), ((0, 0), (0, in_cols_pad - in_cols)))
        u = _in_proj(h, w_in_l)
        conv_out = _conformer(u, conv_dw_w[l], conv_dw_b[l], conv_ln_g[l], conv_ln_b[l])
        ug, wg, qd, kdt, intra, egl = _gdn_pre(u, gdn_conv_w[l], gdn_A_log[l], gdn_dt_bias[l])
        gdn_out = _gdn_rec(ug, wg, qd, kdt, intra, egl, u, gdn_norm_w[l])
        h = _out_ln(conv_out, gdn_out, h, w_out[l].astype(BF16), ln1_g[l], ln1_b[l], alpha)
        h = _ffn_ln(h, w_gate_up[l].astype(BF16), w_down[l].astype(BF16), ln2_g[l], ln2_b[l], alpha)
    return h.reshape(bsz, seq, d)
```

```python
import functools

import jax
import jax.numpy as jnp
from jax import lax
from jax.experimental import pallas as pl
from jax.experimental.pallas import tpu as pltpu

F32 = jnp.float32
BF16 = jnp.bfloat16

LANES = 128
CONV_WIDTH = 31
SHORT_CONV = 4
GDN_HEADS = 8
HEAD_DIM = 128
CHUNK = 64
PAIR = 2 * CHUNK
LN_EPS = 1e-5
RMS_EPS = 1e-6
L2_EPS = 1e-6
HALO = 32
SHORT_HALO = 8

VMEM_LIMIT = 56 * 1024 * 1024


def _sigmoid(x):
    return 1.0 / (1.0 + jnp.exp(-x))


def _silu(x):
    return x * _sigmoid(x)


def _softplus(x):
    return jnp.maximum(x, 0.0) + jnp.log(1.0 + jnp.exp(-jnp.abs(x)))


def _layer_norm(y, g, b):
    mu = jnp.mean(y, axis=-1, keepdims=True)
    d = y - mu
    var = jnp.mean(d * d, axis=-1, keepdims=True)
    return d * lax.rsqrt(var + LN_EPS) * g + b


def _bmm(a, b):
    return lax.dot_general(a.astype(BF16), b.astype(BF16), (((2,), (1,)), ((0,), (0,))),
                           preferred_element_type=F32)


def _bmm_nt(a, b):
    return lax.dot_general(a.astype(BF16), b.astype(BF16), (((2,), (2,)), ((0,), (0,))),
                           preferred_element_type=F32)


def _in_proj_kernel(x_ref, w_ref, o_ref):
    o_ref[...] = jnp.dot(x_ref[...].astype(BF16), w_ref[...], preferred_element_type=F32)


def _in_proj(x, w, tm=1024, tn=896):
    s, d = x.shape
    n = w.shape[1]
    assert s % tm == 0 and n % tn == 0
    return pl.pallas_call(
        _in_proj_kernel,
        grid=(s // tm, n // tn),
        in_specs=[pl.BlockSpec((tm, d), lambda i, j: (i, 0)),
                  pl.BlockSpec((d, tn), lambda i, j: (0, j))],
        out_specs=pl.BlockSpec((tm, tn), lambda i, j: (i, j)),
        out_shape=jax.ShapeDtypeStruct((s, n), F32),
        compiler_params=pltpu.CompilerParams(
            dimension_semantics=("parallel", "parallel"), vmem_limit_bytes=VMEM_LIMIT),
        name="in_proj",
    )(x, w)


def _conformer_kernel(val_ref, gate_ref, w_ref, b_ref, g_ref, beta_ref, o_ref, hbuf, cbuf, *, rows):
    i = pl.program_id(0)
    t, ch = val_ref.shape

    @pl.when(i == 0)
    def _():
        hbuf[0:HALO, :] = jnp.zeros((HALO, ch), F32)

    @pl.when(i > 0)
    def _():
        hbuf[0:HALO, :] = hbuf[t:t + HALO, :]

    hbuf[HALO:HALO + t, :] = val_ref[...] * _sigmoid(gate_ref[...])

    base = HALO - (CONV_WIDTH - 1)
    for cb in range(ch // LANES):
        cols = slice(cb * LANES, (cb + 1) * LANES)

        for r0 in range(0, t, rows):
            win = hbuf[r0:r0 + rows + HALO, cols]
            acc = jnp.broadcast_to(b_ref[:, cols], (rows, LANES))
            for phase in range(8):
                taps = [k for k in range(CONV_WIDTH) if (base + k) % 8 == phase]
                if not taps:
                    continue
                shifted = win if phase == 0 else pltpu.roll(win, rows + HALO - phase, 0)
                for k in taps:
                    a0 = (base + k) // 8 * 8
                    acc = acc + w_ref[k:k + 1, cols] * shifted[a0:a0 + rows]
            cbuf[r0:r0 + rows, cols] = acc

    y = _layer_norm(cbuf[...], g_ref[...], beta_ref[...])
    o_ref[...] = _silu(y).astype(o_ref.dtype)


def _conformer(u, dw_w, dw_b, ln_g, ln_b, t=256, rows=64):
    s = u.shape[0]
    ch = dw_w.shape[1]
    row = lambda a: a.reshape(1, ch).astype(F32)
    full = lambda shape: pl.BlockSpec(shape, lambda i: (0, 0))
    return pl.pallas_call(
        functools.partial(_conformer_kernel, rows=rows),
        grid=(s // t,),
        in_specs=[pl.BlockSpec((t, ch), lambda i: (i, 0)),
                  pl.BlockSpec((t, ch), lambda i: (i, 1)),
                  full((CONV_WIDTH, ch)), full((1, ch)), full((1, ch)), full((1, ch))],
        out_specs=pl.BlockSpec((t, ch), lambda i: (i, 0)),
        out_shape=jax.ShapeDtypeStruct((s, ch), BF16),
        scratch_shapes=[pltpu.VMEM((t + HALO, ch), F32), pltpu.VMEM((t, ch), F32)],
        compiler_params=pltpu.CompilerParams(
            dimension_semantics=("arbitrary",), vmem_limit_bytes=VMEM_LIMIT),
        name="conformer",
    )(u, u, dw_w.astype(F32), row(dw_b), row(ln_g), row(ln_b))


def _gdn_pre_kernel(alog_ref, dtb_ref, q_ref, k_ref, v_ref, ba_ref, wq_ref, wk_ref, wv_ref,
                    m_ref, b_ref, qp_ref, op_ref, egl_ref, xbuf, halo, beta_all, gc_all):
    i = pl.program_id(0)
    h = pl.program_id(1)
    t = q_ref.shape[0]
    nb = t // PAIR

    @pl.when(h == 0)
    def _():
        ba = ba_ref[...]
        beta_all[...] = _sigmoid(ba)
        gc = -jnp.exp(alog_ref[...]) * _softplus(ba + dtb_ref[...])
        rin = lax.broadcasted_iota(jnp.int32, (t, LANES), 0) & (CHUNK - 1)
        d = 1
        while d < CHUNK:
            gc = gc + jnp.where(rin >= d, pltpu.roll(gc, d, 0), 0.0)
            d *= 2
        gc_all[...] = gc

    @pl.when(i == 0)
    def _():
        xbuf[:, 0:SHORT_HALO, :] = jnp.zeros((3, SHORT_HALO, LANES), F32)

    @pl.when(i > 0)
    def _():
        xbuf[:, 0:SHORT_HALO, :] = halo[h]

    xbuf[0, SHORT_HALO:SHORT_HALO + t, :] = q_ref[...]
    xbuf[1, SHORT_HALO:SHORT_HALO + t, :] = k_ref[...]
    xbuf[2, SHORT_HALO:SHORT_HALO + t, :] = v_ref[...]
    halo[h] = xbuf[:, t:t + SHORT_HALO, :]

    def conv_silu(j, wref):
        base = SHORT_HALO - (SHORT_CONV - 1)
        acc = wref[0:1, :] * xbuf[j, pl.ds(base, t), :]
        for kk in range(1, SHORT_CONV):
            acc = acc + wref[kk:kk + 1, :] * xbuf[j, pl.ds(base + kk, t), :]
        return _silu(acc)

    def l2n(a):
        return a * lax.rsqrt(jnp.sum(a * a, axis=-1, keepdims=True) + L2_EPS)

    qn = l2n(conv_silu(0, wq_ref)) * (HEAD_DIM ** -0.5)
    kn = l2n(conv_silu(1, wk_ref))
    v = conv_silu(2, wv_ref)

    lane = lax.broadcasted_iota(jnp.int32, (t, LANES), 1)
    beta = jnp.sum(jnp.where(lane == h, beta_all[...], 0.0), axis=-1, keepdims=True)
    gcol = jnp.sum(jnp.where(lane == h + GDN_HEADS, gc_all[...], 0.0), axis=-1, keepdims=True)
    gc = jnp.broadcast_to(gcol, (t, LANES))

    gc3 = gc.reshape(t // CHUNK, CHUNK, LANES)
    gl3 = gc3[:, CHUNK - 1:CHUNK, :]
    kfac = jnp.exp(gl3 - gc3).reshape(t, LANES)
    egl_ref[...] = jnp.exp(gl3).reshape(t // CHUNK, LANES)
    eg = jnp.exp(gc)

    kb = kn * beta
    vb = v * beta
    kbe = kb * eg
    qd = qn * eg
    kdec = kn * kfac

    blk = lambda a: a.reshape(nb, PAIR, a.shape[-1])
    gcb = blk(gc)
    gct = jnp.stack([gc[b * PAIR:(b + 1) * PAIR, :].T for b in range(nb)], axis=0)
    ii = lax.broadcasted_iota(jnp.int32, (PAIR, PAIR), 0)
    jj = lax.broadcasted_iota(jnp.int32, (PAIR, PAIR), 1)
    same = (ii >= CHUNK) == (jj >= CHUNK)
    causal = same & (ii >= jj)
    strict = same & (ii > jj)
    dec = jnp.exp(jnp.where(causal, gcb - gct, -jnp.inf))

    kn_b = blk(kn).astype(BF16)
    kk = _bmm_nt(blk(kb), kn_b)
    qk = _bmm_nt(blk(qn), kn_b)
    intra = qk * dec

    n = -(kk * jnp.where(strict, dec, 0.0))
    eye = (ii == jj).astype(F32)
    p = eye + n
    npow = n
    steps = 1
    while 2 * steps < CHUNK:
        npow = _bmm(npow, npow)
        p = p + _bmm(p, npow)
        steps *= 2

    rhs = jnp.concatenate([blk(kbe), blk(vb)], axis=-1)
    wu = _bmm(p, rhs).astype(BF16)

    qo = _bmm(intra, wu)
    qp_ref[...] = (qd - qo[:, :, :HEAD_DIM].reshape(t, HEAD_DIM)).astype(qp_ref.dtype)
    op_ref[...] = qo[:, :, HEAD_DIM:].reshape(t, HEAD_DIM)

    kdt = jnp.stack([kdec[b * PAIR:(b + 1) * PAIR, :].T for b in range(nb)], axis=0).astype(BF16)
    first = lax.broadcasted_iota(jnp.int32, (PAIR, 2 * HEAD_DIM), 0) < CHUNK
    zero = jnp.zeros((), BF16)
    mb = jnp.stack([_bmm(kdt, jnp.where(first, wu, zero)),
                    _bmm(kdt, jnp.where(first, zero, wu))], axis=1)
    m_ref[...] = mb[..., :HEAD_DIM].reshape(2 * t, HEAD_DIM).astype(m_ref.dtype)
    b_ref[...] = mb[..., HEAD_DIM:].reshape(2 * t, HEAD_DIM)


def _gdn_pre(u, conv_w, a_log, dt_bias, t=2048):
    s = u.shape[0]
    assert s % t == 0
    nh = GDN_HEADS
    q0 = 2 * 1024 // LANES
    k0 = q0 + nh
    v0 = k0 + nh
    ba0 = q0 + 4 * nh
    col = lambda c0: pl.BlockSpec((t, LANES), lambda i, h: (i, c0 + h))
    wcol = lambda c0: pl.BlockSpec((SHORT_CONV, LANES), lambda i, h: (0, c0 + h))
    gate_row = lambda a: jnp.zeros((1, LANES), F32).at[0, nh:2 * nh].set(a.astype(F32))
    out = lambda rows: pl.BlockSpec((rows, LANES), lambda i, h: (i, h))
    width = nh * HEAD_DIM
    return pl.pallas_call(
        _gdn_pre_kernel,
        grid=(s // t, nh),
        in_specs=[pl.BlockSpec((1, LANES), lambda i, h: (0, 0)),
                  pl.BlockSpec((1, LANES), lambda i, h: (0, 0)),
                  col(q0), col(k0), col(v0),
                  pl.BlockSpec((t, LANES), lambda i, h: (i, ba0)),
                  wcol(0), wcol(nh), wcol(2 * nh)],
        out_specs=[out(2 * t), out(2 * t), out(t), out(t), out(t // CHUNK)],
        out_shape=[jax.ShapeDtypeStruct((2 * s, width), BF16),
                   jax.ShapeDtypeStruct((2 * s, width), F32),
                   jax.ShapeDtypeStruct((s, width), BF16),
                   jax.ShapeDtypeStruct((s, width), F32),
                   jax.ShapeDtypeStruct((s // CHUNK, width), F32)],
        scratch_shapes=[pltpu.VMEM((3, t + SHORT_HALO, LANES), F32),
                        pltpu.VMEM((nh, 3, SHORT_HALO, LANES), F32),
                        pltpu.VMEM((t, LANES), F32), pltpu.VMEM((t, LANES), F32)],
        compiler_params=pltpu.CompilerParams(
            dimension_semantics=("arbitrary", "arbitrary"), vmem_limit_bytes=VMEM_LIMIT),
        name="gdn_pre",
    )(gate_row(a_log), gate_row(dt_bias), u, u, u, u,
      conv_w.astype(F32), conv_w.astype(F32), conv_w.astype(F32))


def _gdn_rec_kernel(m_ref, b_ref, qp_ref, op_ref, egl_ref, z_ref, nw_ref, o_ref, s_ref, obuf):
    i = pl.program_id(0)
    t = qp_ref.shape[0]

    @pl.when(i == 0)
    def _():
        s_ref[...] = jnp.zeros(s_ref.shape, F32)

    for c in range(t // CHUNK):
        rows = slice(c * CHUNK, (c + 1) * CHUNK)
        mrows = slice(c * HEAD_DIM, (c + 1) * HEAD_DIM)
        for h in range(GDN_HEADS):
            cols = slice(h * HEAD_DIM, (h + 1) * HEAD_DIM)
            state = s_ref[h]
            mq = jnp.concatenate([m_ref[mrows, cols], qp_ref[rows, cols]], axis=0)
            r = jnp.dot(mq, state.astype(BF16), preferred_element_type=F32)
            obuf[rows, cols] = r[HEAD_DIM:] + op_ref[rows, cols]
            s_ref[h] = state * egl_ref[c:c + 1, cols] - r[:HEAD_DIM] + b_ref[mrows, cols]

    for h in range(GDN_HEADS):
        cols = slice(h * HEAD_DIM, (h + 1) * HEAD_DIM)
        o = obuf[:, cols]
        y = o * lax.rsqrt(jnp.mean(o * o, axis=-1, keepdims=True) + RMS_EPS) * nw_ref[...]
        o_ref[:, cols] = (y * _silu(z_ref[:, cols])).astype(o_ref.dtype)


def _gdn_rec(m, b, qp, op, egl, u, norm_w, t=512):
    s, width = qp.shape
    z0 = (2 * 1024 + 3 * width) // width
    blk = pl.BlockSpec((t, width), lambda i: (i, 0))
    blk2 = pl.BlockSpec((2 * t, width), lambda i: (i, 0))
    return pl.pallas_call(
        _gdn_rec_kernel,
        grid=(s // t,),
        in_specs=[blk2, blk2, blk, blk,
                  pl.BlockSpec((t // CHUNK, width), lambda i: (i, 0)),
                  pl.BlockSpec((t, width), lambda i: (i, z0)),
                  pl.BlockSpec((1, HEAD_DIM), lambda i: (0, 0))],
        out_specs=blk,
        out_shape=jax.ShapeDtypeStruct((s, width), BF16),
        scratch_shapes=[pltpu.VMEM((GDN_HEADS, HEAD_DIM, HEAD_DIM), F32),
                        pltpu.VMEM((t, width), F32)],
        compiler_params=pltpu.CompilerParams(
            dimension_semantics=("arbitrary",), vmem_limit_bytes=VMEM_LIMIT),
        name="gdn_rec",
    )(m, b, qp, op, egl, u, norm_w.reshape(1, HEAD_DIM).astype(F32))


def _out_ln_kernel(c_ref, g_ref, x_ref, w_ref, lg_ref, lb_ref, o_ref, *, alpha):
    half = c_ref.shape[1]
    mix = jnp.dot(c_ref[...], w_ref[0:half, :], preferred_element_type=F32)
    mix = mix + jnp.dot(g_ref[...], w_ref[half:2 * half, :], preferred_element_type=F32)
    o_ref[...] = _layer_norm(alpha * x_ref[...] + mix, lg_ref[...], lb_ref[...])


def _out_ln(conv_out, gdn_out, x, w_out, ln_g, ln_b, alpha, tm=512):
    s, d = x.shape
    half = conv_out.shape[1]
    row = lambda a: a.reshape(1, d).astype(F32)
    return pl.pallas_call(
        functools.partial(_out_ln_kernel, alpha=alpha),
        grid=(s // tm,),
        in_specs=[pl.BlockSpec((tm, half), lambda i: (i, 0)),
                  pl.BlockSpec((tm, half), lambda i: (i, 0)),
                  pl.BlockSpec((tm, d), lambda i: (i, 0)),
                  pl.BlockSpec((2 * half, d), lambda i: (0, 0)),
                  pl.BlockSpec((1, d), lambda i: (0, 0)),
                  pl.BlockSpec((1, d), lambda i: (0, 0))],
        out_specs=pl.BlockSpec((tm, d), lambda i: (i, 0)),
        out_shape=jax.ShapeDtypeStruct((s, d), F32),
        compiler_params=pltpu.CompilerParams(
            dimension_semantics=("parallel",), vmem_limit_bytes=VMEM_LIMIT),
        name="out_ln",
    )(conv_out, gdn_out, x, w_out, row(ln_g), row(ln_b))


def _ffn_ln_kernel(x_ref, wg_ref, wu_ref, wd_ref, lg_ref, lb_ref, o_ref, xb_ref, acc_ref, *, alpha):
    f = pl.program_id(1)

    @pl.when(f == 0)
    def _():
        xb_ref[...] = x_ref[...].astype(BF16)
        acc_ref[...] = jnp.zeros(acc_ref.shape, F32)

    xb = xb_ref[...]
    gate = jnp.dot(xb, wg_ref[...], preferred_element_type=F32)
    up = jnp.dot(xb, wu_ref[...], preferred_element_type=F32)
    hid = (_silu(gate) * up).astype(BF16)
    acc_ref[...] += jnp.dot(hid, wd_ref[...], preferred_element_type=F32)

    @pl.when(f == pl.num_programs(1) - 1)
    def _():
        o_ref[...] = _layer_norm(alpha * x_ref[...] + acc_ref[...], lg_ref[...], lb_ref[...])


def _ffn_ln(x, w_gate_up, w_down, ln_g, ln_b, alpha, tm=512, tf=512):
    s, d = x.shape
    dff = w_down.shape[0]
    nf = dff // tf
    row = lambda a: a.reshape(1, d).astype(F32)
    return pl.pallas_call(
        functools.partial(_ffn_ln_kernel, alpha=alpha),
        grid=(s // tm, nf),
        in_specs=[pl.BlockSpec((tm, d), lambda i, f: (i, 0)),
                  pl.BlockSpec((d, tf), lambda i, f: (0, f)),
                  pl.BlockSpec((d, tf), lambda i, f: (0, nf + f)),
                  pl.BlockSpec((tf, d), lambda i, f: (f, 0)),
                  pl.BlockSpec((1, d), lambda i, f: (0, 0)),
                  pl.BlockSpec((1, d), lambda i, f: (0, 0))],
        out_specs=pl.BlockSpec((tm, d), lambda i, f: (i, 0)),
        out_shape=jax.ShapeDtypeStruct((s, d), F32),
        scratch_shapes=[pltpu.VMEM((tm, d), BF16), pltpu.VMEM((tm, d), F32)],
        compiler_params=pltpu.CompilerParams(
            dimension_semantics=("parallel", "arbitrary"), vmem_limit_bytes=VMEM_LIMIT),
        name="ffn_ln",
    )(x, w_gate_up, w_gate_up, w_down, row(ln_g), row(ln_b))


def kernel(x, w_in, conv_dw_w, conv_dw_b, conv_ln_g, conv_ln_b, gdn_conv_w, gdn_A_log, gdn_dt_bias,
           gdn_norm_w, w_out, ln1_g, ln1_b, w_gate_up, w_down, ln2_g, ln2_b):
    bsz, seq, d = x.shape
    assert bsz == 1
    depth = w_in.shape[0]
    alpha = (2 * depth) ** 0.25
    in_cols = w_in.shape[2]
    in_cols_pad = -(-in_cols // (7 * LANES)) * (7 * LANES)

    h = x.reshape(seq, d)
    for l in range(depth):
        w_in_l = jnp.pad(w_in[l].astype(BF16), ((0, 0), (0, in_cols_pad - in_cols)))
        u = _in_proj(h, w_in_l)
        conv_out = _conformer(u, conv_dw_w[l], conv_dw_b[l], conv_ln_g[l], conv_ln_b[l])
        m, b, qp, op, egl = _gdn_pre(u, gdn_conv_w[l], gdn_A_log[l], gdn_dt_bias[l])
        gdn_out = _gdn_rec(m, b, qp, op, egl, u, gdn_norm_w[l])
        h = _out_ln(conv_out, gdn_out, h, w_out[l].astype(BF16), ln1_g[l], ln1_b[l], alpha)
        h = _ffn_ln(h, w_gate_up[l].astype(BF16), w_down[l].astype(BF16), ln2_g[l], ln2_b[l], alpha)
    return h.reshape(bsz, seq, d)
```
